```python
import math
import jax, jax.numpy as jnp
from jax import lax
import numpy as np

D_MODEL = 1024
BATCH = 16
SEQ = 4096
DEPTH = 1
DEC_BATCH = 4
DEC_SEQ = 4096
PAST_LEN = 128

PLE_DIM = 256
M_HEADS = 4
M_QK = 128
M_V = 256
M_CHUNK = 64
M_CONV = 5
A_HEADS = 8
A_NOPE = 128
A_ROPE = 64
A_V = 128
Q_LORA = 256
KV_LORA = 256
ROPE_THETA = 10000.0
Q_BLOCK = 128
D_FF = -(-(8 * D_MODEL) // (3 * 256)) * 256
EPS = 1e-6

M_QK_W = 2 * M_HEADS * M_QK
M_V_W = M_HEADS * M_V
M_GATE_W = 4 * M_HEADS
MERGE_W = 2 * D_MODEL
IN_SIZES = (M_QK_W, M_V_W, M_V_W, M_GATE_W, Q_LORA, KV_LORA, A_ROPE, MERGE_W)
IN_W = M_QK_W + 2 * M_V_W + M_GATE_W + Q_LORA + KV_LORA + A_ROPE + MERGE_W

kernel_name = "hybrid_mlstm_mla_encoder"


def _offsets(sizes):
    out, acc = [], 0
    for s in sizes[:-1]:
        acc += s
        out.append(acc)
    return out


def rmsnorm(x, g):
    xf = x.astype(jnp.float32)
    y = xf * lax.rsqrt(jnp.mean(xf * xf, axis=-1, keepdims=True) + EPS)
    return (y * g.astype(jnp.float32)).astype(x.dtype)


def rope(x, cos, sin):
    x1, x2 = jnp.split(x, 2, axis=-1)
    cos = cos.astype(x.dtype)
    sin = sin.astype(x.dtype)
    return jnp.concatenate([x1 * cos - x2 * sin, x1 * sin + x2 * cos], axis=-1)


def centred_conv(u, w):
    S = u.shape[1]
    half = M_CONV // 2
    up = jnp.pad(u, ((0, 0), (half, half), (0, 0)))
    out = up[:, 0:S] * w[0]
    for j in range(1, M_CONV):
        out = out + up[:, j:j + S] * w[j]
    return out


def mlstm_chunkwise(q, k, v, log_i, log_f):
    q, k, v, log_i, log_f = (a.astype(jnp.float32) for a in (q, k, v, log_i, log_f))
    B, H, S, DK = q.shape
    DV = v.shape[-1]
    L = M_CHUNK
    NC = S // L

    def to_chunks(a):
        return jnp.moveaxis(a.reshape(B, H, NC, L, *a.shape[3:]), 2, 0)

    qc, kc, vc, ic, fc = (to_chunks(a) for a in (q, k, v, log_i, log_f))
    tri = jnp.tril(jnp.ones((L, L), dtype=bool))

    def step(carry, inp):
        C, n, m = carry
        qb, kb, vb, ib, fb = inp
        b = jnp.cumsum(fb, axis=-1)
        d_intra = jnp.where(tri, b[..., :, None] - b[..., None, :] + ib[..., None, :], -jnp.inf)
        d_inter = b + m[..., None]
        m_comb = jnp.maximum(d_inter, jnp.max(d_intra, axis=-1))
        w_intra = jnp.exp(d_intra - m_comb[..., None])
        w_inter = jnp.exp(d_inter - m_comb)
        s = jnp.einsum('bhtd,bhsd->bhts', qb, kb) * w_intra
        num = jnp.einsum('bhts,bhsv->bhtv', s, vb) + w_inter[..., None] * jnp.einsum('bhtd,bhdv->bhtv', qb, C)
        den = jnp.sum(s, axis=-1) + w_inter * jnp.einsum('bhtd,bhd->bht', qb, n)
        h = num / jnp.maximum(jnp.abs(den), jnp.exp(-m_comb))[..., None]
        b_last = b[..., -1]
        g = b_last[..., None] - b + ib
        m_new = jnp.maximum(b_last + m, jnp.max(g, axis=-1))
        a_prev = jnp.exp(b_last + m - m_new)
        ka = kb * jnp.exp(g - m_new[..., None])[..., None]
        C_new = a_prev[..., None, None] * C + jnp.einsum('bhsd,bhsv->bhdv', ka, vb)
        n_new = a_prev[..., None] * n + jnp.sum(ka, axis=2)
        return (C_new, n_new, m_new), h

    init = (jnp.zeros((B, H, DK, DV), jnp.float32), jnp.zeros((B, H, DK), jnp.float32),
            jnp.zeros((B, H), jnp.float32))
    _, hs = lax.scan(step, init, (qc, kc, vc, ic, fc))
    return jnp.moveaxis(hs, 0, 2).reshape(B, H, S, DV)


def mla_attention(q_nope, q_rope, k_nope, k_rope, v):
    B, S, H, _ = q_nope.shape
    nb = S // Q_BLOCK
    scale = (A_NOPE + A_ROPE) ** -0.5

    def blocks(a):
        return jnp.moveaxis(a.reshape(B, nb, Q_BLOCK, *a.shape[2:]), 1, 0)

    def one(args):
        qn, qr = args
        s = jnp.einsum('bqhd,bkhd->bhqk', qn, k_nope) + jnp.einsum('bqhr,bkr->bhqk', qr, k_rope)
        p = jax.nn.softmax(s.astype(jnp.float32) * scale, axis=-1).astype(v.dtype)
        return jnp.einsum('bhqk,bkhd->bqhd', p, v)

    o = lax.map(one, (blocks(q_nope), blocks(q_rope)))
    return jnp.moveaxis(o, 0, 1).reshape(B, S, H * A_V)


def encoder_layer(x, p_l, g_mix, w_in, b_gate, conv_w, g_mhead, w_mo, g_q, w_uq, g_kv, w_ukv,
                  w_ao, w_out, g_ffn, w_gate, w_up, w_down, w_ple, w_ple_gate):
    B, S, _ = x.shape
    pos = jnp.arange(S, dtype=jnp.float32)
    inv_freq = ROPE_THETA ** (-jnp.arange(0, A_ROPE, 2, dtype=jnp.float32) / A_ROPE)
    ang = pos[:, None] * inv_freq[None, :]
    cos, sin = jnp.cos(ang), jnp.sin(ang)

    h = rmsnorm(x, g_mix)
    u = h @ w_in
    qk, v, o, gates, c_q, c_kv, k_r, gm = jnp.split(u, _offsets(IN_SIZES), axis=-1)

    qk = jax.nn.silu(centred_conv(qk, conv_w))
    q, k = jnp.split(qk, 2, axis=-1)
    q = q.reshape(B, S, M_HEADS, M_QK).transpose(0, 2, 1, 3) * (M_QK ** -0.5)
    k = k.reshape(B, S, M_HEADS, M_QK).transpose(0, 2, 1, 3)
    vh = v.reshape(B, S, M_HEADS, M_V).transpose(0, 2, 1, 3)
    gates = (gates + b_gate).astype(jnp.float32).reshape(B, S, 4, M_HEADS).transpose(0, 2, 3, 1)
    log_i = jnp.concatenate([gates[:, 0], jnp.flip(gates[:, 1], axis=-1)], axis=1)
    log_f = jax.nn.log_sigmoid(jnp.concatenate([gates[:, 2], jnp.flip(gates[:, 3], axis=-1)], axis=1))
    both = lambda a: jnp.concatenate([a, jnp.flip(a, axis=2)], axis=1)
    hm = mlstm_chunkwise(both(q), both(k), both(vh), log_i, log_f)
    hm = hm[:, :M_HEADS] + jnp.flip(hm[:, M_HEADS:], axis=2)
    hm = rmsnorm(hm.transpose(0, 2, 1, 3), g_mhead.reshape(M_HEADS, M_V)).reshape(B, S, M_V_W)
    y_m = (jax.nn.sigmoid(o) * hm.astype(x.dtype)) @ w_mo

    qa = (rmsnorm(c_q, g_q) @ w_uq).reshape(B, S, A_HEADS, A_NOPE + A_ROPE)
    q_nope = qa[..., :A_NOPE]
    q_rope = rope(qa[..., A_NOPE:], cos[:, None, :], sin[:, None, :])
    kv = (rmsnorm(c_kv, g_kv) @ w_ukv).reshape(B, S, A_HEADS, A_NOPE + A_V)
    k_nope, va = kv[..., :A_NOPE], kv[..., A_NOPE:]
    k_rope = rope(k_r, cos, sin)
    y_a = mla_attention(q_nope, q_rope, k_nope, k_rope, va) @ w_ao

    g_m, g_a = jnp.split(jax.nn.sigmoid(gm), 2, axis=-1)
    x = x + (g_m * y_m + g_a * y_a) @ w_out

    h2 = rmsnorm(x, g_ffn)
    x = x + (jax.nn.silu(h2 @ w_gate) * (h2 @ w_up)) @ w_down

    x = x + jax.nn.sigmoid(x @ w_ple_gate) * (p_l @ w_ple)
    return x


def encoder(x, p, g_mix, w_in, b_gate, conv_w, g_mhead, w_mo, g_q, w_uq, g_kv, w_ukv, w_ao,
            w_out, g_ffn, w_gate, w_up, w_down, w_ple, w_ple_gate, g_final):
    for i in range(DEPTH):
        x = encoder_layer(x, p[i], g_mix[i], w_in[i], b_gate[i], conv_w[i], g_mhead[i], w_mo[i],
                          g_q[i], w_uq[i], g_kv[i], w_ukv[i], w_ao[i], w_out[i], g_ffn[i],
                          w_gate[i], w_up[i], w_down[i], w_ple[i], w_ple_gate[i])
    return rmsnorm(x, g_final)


def setup_inputs(seed: int = 0) -> dict:
    key = jax.random.key(seed)
    ks = jax.random.split(key, 32)

    def nrm(k, shape, scale):
        return jax.random.normal(k, shape, jnp.float32) * scale

    def gain(k, shape):
        return 1.0 + 0.05 * jax.random.normal(k, shape, jnp.float32)

    i_bias = nrm(ks[10], (DEPTH, 2 * M_HEADS), 0.1)
    f_bias = 3.0 + nrm(ks[11], (DEPTH, 2 * M_HEADS), 0.5)
    return {
        "x_prompt": nrm(ks[0], (BATCH, SEQ, D_MODEL), 1.0),
        "x_sample": nrm(ks[1], (DEC_BATCH, DEC_SEQ, D_MODEL), 1.0),
        "p_prompt": nrm(ks[2], (DEPTH, BATCH, SEQ, PLE_DIM), 1.0),
        "p_sample": nrm(ks[3], (DEPTH, DEC_BATCH, DEC_SEQ, PLE_DIM), 1.0),
        "g_mix": gain(ks[4], (DEPTH, D_MODEL)),
        "w_in": nrm(ks[5], (DEPTH, D_MODEL, IN_W), D_MODEL ** -0.5),
        "b_gate": jnp.concatenate([i_bias, f_bias], axis=-1),
        "conv_w": nrm(ks[6], (DEPTH, M_CONV, M_QK_W), M_CONV ** -0.5),
        "g_mhead": gain(ks[7], (DEPTH, M_V_W)),
        "w_mo": nrm(ks[8], (DEPTH, M_V_W, D_MODEL), M_V_W ** -0.5),
        "g_q": gain(ks[9], (DEPTH, Q_LORA)),
        "w_uq": nrm(ks[12], (DEPTH, Q_LORA, A_HEADS * (A_NOPE + A_ROPE)), Q_LORA ** -0.5),
        "g_kv": gain(ks[13], (DEPTH, KV_LORA)),
        "w_ukv": nrm(ks[14], (DEPTH, KV_LORA, A_HEADS * (A_NOPE + A_V)), KV_LORA ** -0.5),
        "w_ao": nrm(ks[15], (DEPTH, A_HEADS * A_V, D_MODEL), (A_HEADS * A_V) ** -0.5),
        "w_out": nrm(ks[16], (DEPTH, D_MODEL, D_MODEL), D_MODEL ** -0.5),
        "g_ffn": gain(ks[17], (DEPTH, D_MODEL)),
        "w_gate": nrm(ks[18], (DEPTH, D_MODEL, D_FF), D_MODEL ** -0.5),
        "w_up": nrm(ks[19], (DEPTH, D_MODEL, D_FF), D_MODEL ** -0.5),
        "w_down": nrm(ks[20], (DEPTH, D_FF, D_MODEL), D_FF ** -0.5),
        "w_ple": nrm(ks[21], (DEPTH, PLE_DIM, D_MODEL), PLE_DIM ** -0.5),
        "w_ple_gate": nrm(ks[22], (DEPTH, D_MODEL, D_MODEL), D_MODEL ** -0.5),
        "g_final": gain(ks[23], (D_MODEL,)),
    }


def reference(x_prompt, x_sample, p_prompt, p_sample, g_mix, w_in, b_gate, conv_w, g_mhead, w_mo,
              g_q, w_uq, g_kv, w_ukv, w_ao, w_out, g_ffn, w_gate, w_up, w_down, w_ple, w_ple_gate,
              g_final):
    y_prompt = encoder(x_prompt, p_prompt, g_mix, w_in, b_gate, conv_w, g_mhead, w_mo, g_q, w_uq,
                       g_kv, w_ukv, w_ao, w_out, g_ffn, w_gate, w_up, w_down, w_ple, w_ple_gate, g_final)
    y_sample = encoder(x_sample, p_sample, g_mix, w_in, b_gate, conv_w, g_mhead, w_mo, g_q, w_uq,
                       g_kv, w_ukv, w_ao, w_out, g_ffn, w_gate, w_up, w_down, w_ple, w_ple_gate, g_final)
    return (y_prompt, y_sample)
```

```python
import functools
import math

import jax
import jax.numpy as jnp
from jax import lax
from jax.experimental import pallas as pl
from jax.experimental.pallas import tpu as pltpu

F32 = jnp.float32
BF = jnp.bfloat16

D_MODEL = 1024
PLE_DIM = 256
M_HEADS = 4
M_QK = 128
M_V = 256
M_CONV = 5
A_HEADS = 8
A_NOPE = 128
A_ROPE = 64
A_V = 128
Q_LORA = 256
KV_LORA = 256
ROPE_THETA = 10000.0
D_FF = 2816
EPS = 1e-6

M_QK_W = 2 * M_HEADS * M_QK
M_V_W = M_HEADS * M_V
M_GATE_W = 4 * M_HEADS
MERGE_W = 2 * D_MODEL

LANES = 128
A_QK_PAD = 2 * LANES
M_CHUNK = 128
CONV_HALO = 8
GATE_ROWS = 8

Q_SCALE = (A_NOPE + A_ROPE) ** -0.5 * math.log2(math.e)

VMEM_LIMIT = 56 * 1024 * 1024

_C_QK, _C_V, _C_O, _C_GM, _C_CQ, _C_CKV, _C_KR, _C_END = 0, 1024, 2048, 3072, 5120, 5376, 5632, 5888


def _dot(a, b):
    return jnp.dot(a, b, preferred_element_type=F32)


def _dot_nt(a, b):
    return lax.dot_general(a, b, (((1,), (1,)), ((), ())), preferred_element_type=F32)


def _rms(x, g):
    return x * lax.rsqrt(jnp.mean(x * x, axis=-1, keepdims=True) + EPS) * g


def _const_spec(shape):
    nd = len(shape)
    return pl.BlockSpec(shape, lambda *_: (0,) * nd, pipeline_mode=pl.Buffered(1))


def _inproj_kernel(x_ref, cos_ref, sin_ref, gmix_ref, w_ref, wgt_ref, bg_ref, gq_ref, gkv_ref,
                   wuqm_ref, wuqs_ref, wukv_ref,
                   qk_ref, vm_ref, o_ref, gm_ref, gt_ref, qcat_ref, kn_ref, va_ref, krp_ref):
    hb = _rms(x_ref[...], gmix_ref[...]).astype(BF)
    qk_ref[...] = _dot(hb, w_ref[:, _C_QK:_C_V]).astype(BF)
    vm_ref[...] = _dot(hb, w_ref[:, _C_V:_C_O]).astype(BF)
    o_ref[...] = _dot(hb, w_ref[:, _C_O:_C_GM]).astype(BF)
    gm_ref[...] = _dot(hb, w_ref[:, _C_GM:_C_CQ]).astype(BF)
    cq = _dot(hb, w_ref[:, _C_CQ:_C_CKV])
    ckv = _dot(hb, w_ref[:, _C_CKV:_C_KR])
    kr2 = _dot(hb, w_ref[:, _C_KR:_C_END])
    cos = cos_ref[...]
    sin = sin_ref[...]
    krp_ref[...] = (kr2[:, :LANES] * cos + kr2[:, LANES:] * sin).astype(BF)

    g = _dot_nt(wgt_ref[...], hb) + bg_ref[...]
    row = lax.broadcasted_iota(jnp.int32, g.shape, 0) % GATE_ROWS
    logsig = jnp.minimum(g, 0.0) - jnp.log1p(jnp.exp(-jnp.abs(g)))
    gt_ref[...] = jnp.where((row == 2) | (row == 3), logsig, g)

    cqn = _rms(cq, gq_ref[...]).astype(BF)
    qm = _dot(cqn, wuqm_ref[...])
    qs = _dot(cqn, wuqs_ref[...])
    for h in range(A_HEADS):
        a = h * A_QK_PAD
        qcat_ref[:, a:a + LANES] = (qm[:, a:a + LANES] * Q_SCALE).astype(BF)
        rot = qm[:, a + LANES:a + A_QK_PAD] * cos + qs[:, h * LANES:(h + 1) * LANES] * sin
        qcat_ref[:, a + LANES:a + A_QK_PAD] = (rot * Q_SCALE).astype(BF)

    ckvn = _rms(ckv, gkv_ref[...]).astype(BF)
    kv = _dot(ckvn, wukv_ref[...])
    kn_ref[...] = kv[:, :A_HEADS * A_NOPE].astype(BF)
    va_ref[...] = kv[:, A_HEADS * A_NOPE:].astype(BF)


def _inproj(x2, seq, cos128, sin128, wts, tm=256):
    n = x2.shape[0]
    nseq = seq // tm
    row = lambda w: pl.BlockSpec((tm, w), lambda i: (i, 0))
    out_shapes = [
        jax.ShapeDtypeStruct((n, M_QK_W), BF),
        jax.ShapeDtypeStruct((n, M_V_W), BF),
        jax.ShapeDtypeStruct((n, M_V_W), BF),
        jax.ShapeDtypeStruct((n, MERGE_W), BF),
        jax.ShapeDtypeStruct((M_HEADS * GATE_ROWS, n), F32),
        jax.ShapeDtypeStruct((n, A_HEADS * A_QK_PAD), BF),
        jax.ShapeDtypeStruct((n, A_HEADS * A_NOPE), BF),
        jax.ShapeDtypeStruct((n, A_HEADS * A_V), BF),
        jax.ShapeDtypeStruct((n, LANES), BF),
    ]
    out_specs = [row(M_QK_W), row(M_V_W), row(M_V_W), row(MERGE_W),
                 pl.BlockSpec((M_HEADS * GATE_ROWS, tm), lambda i: (0, i)),
                 row(A_HEADS * A_QK_PAD), row(A_HEADS * A_NOPE), row(A_HEADS * A_V), row(LANES)]
    tab = pl.BlockSpec((tm, LANES), lambda i: (i % nseq, 0))
    in_specs = [row(D_MODEL), tab, tab] + [_const_spec(w.shape) for w in wts]
    return pl.pallas_call(
        _inproj_kernel,
        grid=(n // tm,),
        in_specs=in_specs,
        out_specs=out_specs,
        out_shape=out_shapes,
        compiler_params=pltpu.CompilerParams(
            dimension_semantics=("parallel",), vmem_limit_bytes=VMEM_LIMIT),
        name="inproj",
    )(x2, cos128, sin128, *wts)


def _split3(x):
    hi = x.astype(BF).astype(F32)
    r1 = x - hi
    mid = r1.astype(BF).astype(F32)
    lo = (r1 - mid).astype(BF).astype(F32)
    return hi, mid, lo


def _mlstm_kernel(q_ref, k_ref, v_ref, o_ref, g_ref, cwq_ref, cwk_ref, gmh_ref, out_ref,
                  qf, kf, qs, kst, hsum, c_scr, n_scr, m_scr, *, seq):
    L = M_CHUNK
    nc = seq // L

    zero_halo = jnp.zeros((CONV_HALO, M_QK), F32)
    for buf in (qf, kf):
        buf[0:CONV_HALO, :] = zero_halo
        buf[seq + CONV_HALO:seq + 2 * CONV_HALO, :] = zero_halo

    def upcast(c, carry):
        r = pl.multiple_of(c * L, L)
        qf[pl.ds(r + CONV_HALO, L), :] = q_ref[pl.ds(r, L), :].astype(F32)
        kf[pl.ds(r + CONV_HALO, L), :] = k_ref[pl.ds(r, L), :].astype(F32)
        return carry

    lax.fori_loop(0, nc, upcast, 0)

    def conv_silu(buf, w_ref, r):
        win = buf[pl.ds(r, L + 2 * CONV_HALO), :]
        first = CONV_HALO - M_CONV // 2
        acc = win[first:first + L, :] * w_ref[0:1, :]
        for j in range(1, M_CONV):
            acc = acc + win[first + j:first + j + L, :] * w_ref[j:j + 1, :]
        return acc * jax.nn.sigmoid(acc)

    def conv(c, carry):
        r = pl.multiple_of(c * L, L)
        qs[pl.ds(r, L), :] = (conv_silu(qf, cwq_ref, r) * (M_QK ** -0.5)).astype(BF)
        kst[:, pl.ds(r, L)] = conv_silu(kf, cwk_ref, r).T.astype(BF)
        return carry

    lax.fori_loop(0, nc, conv, 0)

    ti = lax.broadcasted_iota(jnp.int32, (L, L), 0)
    si = lax.broadcasted_iota(jnp.int32, (L, L), 1)
    lower = si <= ti
    upper = si >= ti
    lower_b = lower.astype(BF)
    upper_b = upper.astype(BF)
    tri_mask = (lower, upper)
    tri3 = tuple(jnp.concatenate([m, m, m], axis=1) for m in (lower_b, upper_b))
    tri3_t = tuple(jnp.concatenate([m, m, m], axis=0) for m in (upper_b, lower_b))

    c_scr[...] = jnp.zeros(c_scr.shape, F32)
    n_scr[...] = jnp.zeros(n_scr.shape, F32)
    m_scr[...] = jnp.zeros(m_scr.shape, F32)

    def step(d, c, accumulate):
        r = pl.multiple_of(c * L, L)
        q = qs[pl.ds(r, L), :]
        kt = kst[:, pl.ds(r, L)]
        v = v_ref[pl.ds(r, L), :]
        irow = g_ref[d:d + 1, pl.ds(r, L)]
        frow = g_ref[2 + d:3 + d, pl.ds(r, L)]

        f3 = jnp.concatenate(_split3(frow), axis=1)
        bcol = _dot_nt(tri3[d], jnp.broadcast_to(f3, (LANES, 3 * L)).astype(BF))
        brow = _dot(jnp.broadcast_to(f3, (8, 3 * L)).astype(BF), tri3_t[d])[0:1, :]
        arow = irow - brow

        m = m_scr[d][0:1, :]
        dmat = jnp.where(tri_mask[d], bcol + arow, -jnp.inf)
        d_inter = bcol + m
        m_comb = jnp.maximum(d_inter, jnp.max(dmat, axis=1, keepdims=True))
        w_intra = jnp.exp(dmat - m_comb)
        w_inter = jnp.exp(d_inter - m_comb)

        s = _dot(q, kt) * w_intra
        cn = jnp.concatenate([c_scr[d].astype(BF), n_scr[d].astype(BF)], axis=1)
        inter = _dot(q, cn)
        w_inter2 = jnp.concatenate([w_inter, w_inter], axis=1)
        num = _dot(s.astype(BF), v) + w_inter2 * inter[:, :M_V]
        den = jnp.sum(s, axis=1, keepdims=True) + w_inter * inter[:, M_V:]
        inv = 1.0 / jnp.maximum(jnp.abs(den), jnp.exp(-m_comb))
        h = num * jnp.concatenate([inv, inv], axis=1)
        if accumulate:
            hsum[pl.ds(r, L), :] = hsum[pl.ds(r, L), :] + h
        else:
            hsum[pl.ds(r, L), :] = h

        b_last = bcol[L - 1:L, :] if d == 0 else bcol[0:1, :]
        grow = b_last + arow
        m_new = jnp.maximum(b_last + m, jnp.max(grow, axis=1, keepdims=True))
        a_prev = jnp.exp(b_last + m - m_new)
        ktw = kt.astype(F32) * jnp.exp(grow - m_new)
        c_scr[d] = jnp.concatenate([a_prev, a_prev], axis=1) * c_scr[d] + _dot(ktw.astype(BF), v)
        n_scr[d] = a_prev * n_scr[d] + jnp.sum(ktw, axis=1, keepdims=True)
        m_scr[d] = jnp.broadcast_to(m_new, (8, LANES))

    def scan_body(accumulate, c, carry):
        step(0, c, accumulate)
        step(1, nc - 1 - c, accumulate)
        return carry

    lax.fori_loop(0, nc // 2, functools.partial(scan_body, False), 0)
    lax.fori_loop(nc // 2, nc, functools.partial(scan_body, True), 0)

    def finish(c, carry):
        r = pl.multiple_of(c * L, L)
        y = _rms(hsum[pl.ds(r, L), :], gmh_ref[...])
        gate = jax.nn.sigmoid(o_ref[pl.ds(r, L), :].astype(F32))
        out_ref[pl.ds(r, L), :] = (gate * y).astype(BF)
        return carry

    lax.fori_loop(0, nc, finish, 0)


def _mlstm(qk, vm, o, gt, conv_w, g_mhead, batch, seq):
    n = batch * seq
    kern = functools.partial(_mlstm_kernel, seq=seq)
    in_specs = [
        pl.BlockSpec((seq, M_QK), lambda b, h: (b, h)),
        pl.BlockSpec((seq, M_QK), lambda b, h: (b, M_HEADS + h)),
        pl.BlockSpec((seq, M_V), lambda b, h: (b, h)),
        pl.BlockSpec((seq, M_V), lambda b, h: (b, h)),
        pl.BlockSpec((GATE_ROWS, seq), lambda b, h: (h, b)),
        pl.BlockSpec((M_CONV, M_QK), lambda b, h: (0, h)),
        pl.BlockSpec((M_CONV, M_QK), lambda b, h: (0, M_HEADS + h)),
        pl.BlockSpec((1, M_V), lambda b, h: (0, h)),
    ]
    scratch = [
        pltpu.VMEM((seq + 2 * CONV_HALO, M_QK), F32),
        pltpu.VMEM((seq + 2 * CONV_HALO, M_QK), F32),
        pltpu.VMEM((seq, M_QK), BF),
        pltpu.VMEM((M_QK, seq), BF),
        pltpu.VMEM((seq, M_V), F32),
        pltpu.VMEM((2, M_QK, M_V), F32),
        pltpu.VMEM((2, M_QK, LANES), F32),
        pltpu.VMEM((2, 8, LANES), F32),
    ]
    return pl.pallas_call(
        kern,
        grid=(batch, M_HEADS),
        in_specs=in_specs,
        out_specs=pl.BlockSpec((seq, M_V), lambda b, h: (b, h)),
        out_shape=jax.ShapeDtypeStruct((n, M_V_W), BF),
        scratch_shapes=scratch,
        compiler_params=pltpu.CompilerParams(
            dimension_semantics=("parallel", "parallel"), vmem_limit_bytes=VMEM_LIMIT),
        name="mlstm",
    )(qk, qk, vm, o, gt, conv_w, conv_w, g_mhead)


def _attn_kernel(q_ref, kn_ref, kr_ref, v_ref, o_ref, kcat, m_s, l_s, acc_s, *, seq, tk):
    @pl.when(pl.program_id(2) == 0)
    def _():
        kcat[:, 0:LANES] = kn_ref[...]
        kcat[:, LANES:A_QK_PAD] = kr_ref[...]

    q = q_ref[...]
    m_s[...] = jnp.full(m_s.shape, -jnp.inf, F32)
    l_s[...] = jnp.zeros(l_s.shape, F32)
    acc_s[...] = jnp.zeros(acc_s.shape, F32)

    def body(j, carry):
        r = pl.multiple_of(j * tk, tk)
        s = _dot_nt(q, kcat[pl.ds(r, tk), :])
        m_prev = m_s[...]
        m_new = jnp.maximum(m_prev, jnp.max(s, axis=1, keepdims=True))
        alpha = jnp.exp2(m_prev - m_new)
        p = jnp.exp2(s - jnp.concatenate([m_new] * (tk // LANES), axis=1))
        l_s[...] = alpha * l_s[...] + jnp.sum(p, axis=1, keepdims=True)
        acc_s[...] = alpha * acc_s[...] + _dot(p.astype(BF), v_ref[pl.ds(r, tk), :])
        m_s[...] = m_new
        return carry

    lax.fori_loop(0, seq // tk, body, 0)
    o_ref[...] = (acc_s[...] / l_s[...]).astype(BF)


def _attention(qcat, kn, krp, va, batch, seq, tq=512, tk=512):
    n = batch * seq
    nq = seq // tq
    kern = functools.partial(_attn_kernel, seq=seq, tk=tk)
    in_specs = [
        pl.BlockSpec((tq, A_QK_PAD), lambda b, h, i: (b * nq + i, h)),
        pl.BlockSpec((seq, A_NOPE), lambda b, h, i: (b, h)),
        pl.BlockSpec((seq, LANES), lambda b, h, i: (b, 0)),
        pl.BlockSpec((seq, A_V), lambda b, h, i: (b, h)),
    ]
    scratch = [
        pltpu.VMEM((seq, A_QK_PAD), BF),
        pltpu.VMEM((tq, LANES), F32),
        pltpu.VMEM((tq, LANES), F32),
        pltpu.VMEM((tq, A_V), F32),
    ]
    return pl.pallas_call(
        kern,
        grid=(batch, A_HEADS, nq),
        in_specs=in_specs,
        out_specs=pl.BlockSpec((tq, A_V), lambda b, h, i: (b * nq + i, h)),
        out_shape=jax.ShapeDtypeStruct((n, A_HEADS * A_V), BF),
        scratch_shapes=scratch,
        compiler_params=pltpu.CompilerParams(
            dimension_semantics=("parallel", "parallel", "arbitrary"),
            vmem_limit_bytes=VMEM_LIMIT),
        name="attention",
    )(qcat, kn, krp, va)


def _merge_kernel(x_ref, hm_ref, at_ref, gm_ref, wmo_ref, wao_ref, wout_ref, x1_ref):
    y_m = _dot(hm_ref[...], wmo_ref[...])
    y_a = _dot(at_ref[...], wao_ref[...])
    gate = jax.nn.sigmoid(gm_ref[...].astype(F32))
    z = gate[:, :D_MODEL] * y_m + gate[:, D_MODEL:] * y_a
    x1_ref[...] = x_ref[...] + _dot(z.astype(BF), wout_ref[...])


def _merge(x2, hmg, att, gm, wts, tm=512):
    n = x2.shape[0]
    row = lambda w: pl.BlockSpec((tm, w), lambda i: (i, 0))
    return pl.pallas_call(
        _merge_kernel,
        grid=(n // tm,),
        in_specs=[row(D_MODEL), row(M_V_W), row(A_HEADS * A_V), row(MERGE_W)]
        + [_const_spec(w.shape) for w in wts],
        out_specs=row(D_MODEL),
        out_shape=jax.ShapeDtypeStruct((n, D_MODEL), F32),
        compiler_params=pltpu.CompilerParams(
            dimension_semantics=("parallel",), vmem_limit_bytes=VMEM_LIMIT),
        name="merge",
    )(x2, hmg, att, gm, *wts)


def _ffn_kernel(x1_ref, p_ref, gffn_ref, wg_ref, wu_ref, wd_ref, wpg_ref, wp_ref, gfin_ref, y_ref):
    x1 = x1_ref[...]
    h2 = _rms(x1, gffn_ref[...]).astype(BF)
    a = _dot(h2, wg_ref[...])
    u = _dot(h2, wu_ref[...])
    act = (a * jax.nn.sigmoid(a) * u).astype(BF)
    x2 = x1 + _dot(act, wd_ref[...])
    pg = _dot(x2.astype(BF), wpg_ref[...])
    pp = _dot(p_ref[...].astype(BF), wp_ref[...])
    x3 = x2 + jax.nn.sigmoid(pg) * pp
    y_ref[...] = _rms(x3, gfin_ref[...])


def _ffn(x1, p2, wts, tm=256):
    n = x1.shape[0]
    row = lambda w: pl.BlockSpec((tm, w), lambda i: (i, 0))
    return pl.pallas_call(
        _ffn_kernel,
        grid=(n // tm,),
        in_specs=[row(D_MODEL), row(PLE_DIM)] + [_const_spec(w.shape) for w in wts],
        out_specs=row(D_MODEL),
        out_shape=jax.ShapeDtypeStruct((n, D_MODEL), F32),
        compiler_params=pltpu.CompilerParams(
            dimension_semantics=("parallel",), vmem_limit_bytes=VMEM_LIMIT),
        name="ffn",
    )(x1, p2, *wts)


def _pack_weights(g_mix, w_in, b_gate, conv_w, g_mhead, w_mo, g_q, w_uq, g_kv, w_ukv, w_ao,
                  w_out, g_ffn, w_gate, w_up, w_down, w_ple, w_ple_gate, g_final):
    half = A_ROPE // 2
    o_qk = 0
    o_v = o_qk + M_QK_W
    o_o = o_v + M_V_W
    o_g = o_o + M_V_W
    o_cq = o_g + M_GATE_W
    o_ckv = o_cq + Q_LORA
    o_kr = o_ckv + KV_LORA
    o_gm = o_kr + A_ROPE
    w_kr = w_in[:, o_kr:o_gm]
    w_kr_swap = jnp.concatenate([w_kr[:, half:], w_kr[:, :half]], axis=1)
    zpad = jnp.zeros((D_MODEL, LANES - A_ROPE), w_in.dtype)
    w_big = jnp.concatenate([
        w_in[:, o_qk:o_g], w_in[:, o_gm:o_gm + MERGE_W], w_in[:, o_cq:o_kr],
        w_kr, zpad, w_kr_swap, zpad], axis=1).astype(BF)

    w_g = w_in[:, o_g:o_cq].reshape(D_MODEL, 4, M_HEADS)
    w_gt = jnp.transpose(w_g, (2, 1, 0))
    w_gt = jnp.concatenate([w_gt, jnp.zeros_like(w_gt)], axis=1).reshape(M_HEADS * GATE_ROWS, D_MODEL)
    b_g = jnp.transpose(b_gate.reshape(4, M_HEADS), (1, 0))
    b_g = jnp.concatenate([b_g, jnp.zeros_like(b_g)], axis=1).reshape(M_HEADS * GATE_ROWS, 1)

    wq = w_uq.reshape(Q_LORA, A_HEADS, A_NOPE + A_ROPE)
    wq_rope = wq[:, :, A_NOPE:]
    wq_main = jnp.concatenate(
        [wq, jnp.zeros((Q_LORA, A_HEADS, A_QK_PAD - A_NOPE - A_ROPE), w_uq.dtype)], axis=2)
    wq_swap = jnp.concatenate(
        [wq_rope[:, :, half:], wq_rope[:, :, :half],
         jnp.zeros((Q_LORA, A_HEADS, LANES - A_ROPE), w_uq.dtype)], axis=2)
    wkv = w_ukv.reshape(KV_LORA, A_HEADS, A_NOPE + A_V)
    wkv_perm = jnp.concatenate([wkv[:, :, :A_NOPE].reshape(KV_LORA, -1),
                                wkv[:, :, A_NOPE:].reshape(KV_LORA, -1)], axis=1)

    inproj_w = (g_mix.reshape(1, -1), w_big, w_gt.astype(BF), b_g.astype(F32),
                g_q.reshape(1, -1), g_kv.reshape(1, -1),
                wq_main.reshape(Q_LORA, -1).astype(BF), wq_swap.reshape(Q_LORA, -1).astype(BF),
                wkv_perm.astype(BF))
    mlstm_w = (conv_w, g_mhead.reshape(1, -1))
    merge_w = (w_mo.astype(BF), w_ao.astype(BF), w_out.astype(BF))
    ffn_w = (g_ffn.reshape(1, -1), w_gate.astype(BF), w_up.astype(BF), w_down.astype(BF),
             w_ple_gate.astype(BF), w_ple.astype(BF), g_final.reshape(1, -1))
    return inproj_w, mlstm_w, merge_w, ffn_w


def _rope_tables(seq):
    half = A_ROPE // 2
    pos = jnp.arange(seq, dtype=F32)
    inv_freq = ROPE_THETA ** (-jnp.arange(0, A_ROPE, 2, dtype=F32) / A_ROPE)
    ang = pos[:, None] * inv_freq[None, :]
    cos, sin = jnp.cos(ang), jnp.sin(ang)
    zpad = jnp.zeros((seq, LANES - A_ROPE), F32)
    return (jnp.concatenate([cos, cos, zpad], axis=1), jnp.concatenate([-sin, sin, zpad], axis=1))


def _encoder(x, p, packed):
    batch, seq, _ = x.shape
    inproj_w, mlstm_w, merge_w, ffn_w = packed
    x2 = x.reshape(batch * seq, D_MODEL)
    p2 = p.reshape(batch * seq, PLE_DIM)
    cos128, sin128 = _rope_tables(seq)
    qk, vm, o, gm, gt, qcat, kn, va, krp = _inproj(x2, seq, cos128, sin128, inproj_w)
    hmg = _mlstm(qk, vm, o, gt, *mlstm_w, batch, seq)
    att = _attention(qcat, kn, krp, va, batch, seq)
    x1 = _merge(x2, hmg, att, gm, merge_w)
    y = _ffn(x1, p2, ffn_w)
    return y.reshape(batch, seq, D_MODEL)


def kernel(x_prompt, x_sample, p_prompt, p_sample, g_mix, w_in, b_gate, conv_w, g_mhead, w_mo, g_q, w_uq, g_kv, w_ukv, w_ao, w_out, g_ffn, w_gate, w_up, w_down, w_ple, w_ple_gate, g_final):
    assert w_in.shape[0] == 1, "single-layer encoder"
    packed = _pack_weights(g_mix[0], w_in[0], b_gate[0], conv_w[0], g_mhead[0], w_mo[0], g_q[0],
                           w_uq[0], g_kv[0], w_ukv[0], w_ao[0], w_out[0], g_ffn[0], w_gate[0],
                           w_up[0], w_down[0], w_ple[0], w_ple_gate[0], g_final)
    y_prompt = _encoder(x_prompt, p_prompt[0], packed)
    y_sample = _encoder(x_sample, p_sample[0], packed)
    return (y_prompt, y_sample)
```
